```python
import jax, jax.numpy as jnp
from jax import lax
import numpy as np

D_MODEL = 2048
BATCH = 4
SEQ = 2048
DEPTH = 4

N_HEADS = 16
HEAD_DIM = D_MODEL // N_HEADS
MIX_WIDTH = N_HEADS * HEAD_DIM
DIL_GROUPS = ((128, 1), (512, 4), (2048, 16))
N_GROUPS = len(DIL_GROUPS)
ROPE_THETA = 500000.0
ROT_DIM = HEAD_DIM // 4
CHUNK = 128
N_CH_GROUPS = 16
CH_GROUP = MIX_WIDTH // N_CH_GROUPS
RMS_EPS = 1e-6
LN_EPS = 1e-5
NEG_INF = -1e30
N_A = (DEPTH + 1) // 2
N_B = DEPTH // 2
A_IN = N_GROUPS * 3 * MIX_WIDTH + MIX_WIDTH
B_IN = 3 * MIX_WIDTH

kernel_name = "hybrid_dilated_attn_chunked_gmlp"


def rmsnorm(x, g):
    xf = x.astype(jnp.float32)
    y = xf * lax.rsqrt(jnp.mean(xf * xf, axis=-1, keepdims=True) + RMS_EPS)
    return (y * g.astype(jnp.float32)).astype(x.dtype)


def partial_rope(t, positions):
    half = ROT_DIM // 2
    inv_freq = ROPE_THETA ** (-(jnp.arange(half, dtype=jnp.float32) * 2.0) / ROT_DIM)
    ang = positions.astype(jnp.float32)[..., None] * inv_freq
    cos = jnp.cos(ang)[:, :, None, :]
    sin = jnp.sin(ang)[:, :, None, :]
    tf = t.astype(jnp.float32)
    x1, x2 = tf[..., :half], tf[..., half:ROT_DIM]
    out = jnp.concatenate([x1 * cos - x2 * sin, x2 * cos + x1 * sin, tf[..., ROT_DIM:]], axis=-1)
    return out.astype(t.dtype)


def dilated_window_attention(q, k, v, window, dilation):
    b, s, h, e = q.shape
    blk = window // dilation
    span = blk * dilation
    s_pad = -(-s // span) * span
    L = s_pad // dilation
    nb = L // blk
    pad = ((0, 0), (0, s_pad - s), (0, 0), (0, 0))

    def to_blocks(t):
        t = jnp.pad(t.astype(jnp.float32), pad).reshape(b, L, dilation, h, e)
        return t.transpose(0, 2, 1, 3, 4).reshape(b, dilation, nb, blk, h, e)

    def with_prev(t):
        prev = jnp.pad(t, ((0, 0), (0, 0), (1, 0), (0, 0), (0, 0), (0, 0)))[:, :, :-1]
        return jnp.concatenate([prev, t], axis=3)

    qb = to_blocks(q)
    kk = with_prev(to_blocks(k))
    vv = with_prev(to_blocks(v))
    scores = jnp.einsum('brnqhe,brnkhe->brnhqk', qb, kk) * (e ** -0.5)
    qi = jnp.arange(blk)[:, None]
    kj = jnp.arange(2 * blk)[None, :]
    dist = blk + qi - kj
    band = (dist >= 0) & (dist <= blk)
    has_prev = (jnp.arange(nb) > 0)[:, None, None] | (kj >= blk)[None]
    mask = band[None] & has_prev
    scores = jnp.where(mask[:, None], scores, NEG_INF)
    lse = jax.nn.logsumexp(scores, axis=-1)
    p = jnp.exp(scores - lse[..., None])
    out = jnp.einsum('brnhqk,brnkhe->brnqhe', p, vv)
    out = out.reshape(b, dilation, L, h, e).transpose(0, 2, 1, 3, 4).reshape(b, s_pad, h, e)[:, :s]
    lse = lse.transpose(0, 1, 2, 4, 3).reshape(b, dilation, L, h)
    lse = lse.transpose(0, 2, 1, 3).reshape(b, s_pad, h)[:, :s]
    return out, lse


def dilated_attention_layer(x, positions, norm_g, w_in, w_out):
    b, s, _ = x.shape
    hn = rmsnorm(x, norm_g)
    proj = hn @ w_in
    n_qkv = N_GROUPS * 3 * MIX_WIDTH
    qkv = proj[..., :n_qkv].reshape(b, s, N_GROUPS, 3, N_HEADS, HEAD_DIM)
    gate = proj[..., n_qkv:]
    outs, lses = [], []
    for g, (window, dilation) in enumerate(DIL_GROUPS):
        q = partial_rope(qkv[:, :, g, 0], positions)
        k = partial_rope(qkv[:, :, g, 1], positions)
        o, l = dilated_window_attention(q, k, qkv[:, :, g, 2], window, dilation)
        outs.append(o)
        lses.append(l)
    wts = jax.nn.softmax(jnp.stack(lses, axis=0), axis=0)
    mixed = jnp.sum(wts[..., None] * jnp.stack(outs, axis=0), axis=0)
    y = mixed.reshape(b, s, MIX_WIDTH).astype(x.dtype) * jax.nn.silu(gate)
    return x + y @ w_out


def chunked_spatial_gating_layer(x, norm_g, w_in, ln_g, ln_b, w_s, b_s, w_out):
    b, s, _ = x.shape
    hn = rmsnorm(x, norm_g)
    u, v, z = jnp.split(hn @ w_in, 3, axis=-1)
    u = jax.nn.gelu(u, approximate=False)
    v = jax.nn.gelu(v, approximate=False)
    vf = v.astype(jnp.float32)
    mu = jnp.mean(vf, axis=-1, keepdims=True)
    var = jnp.mean(jnp.square(vf - mu), axis=-1, keepdims=True)
    vn = (vf - mu) * lax.rsqrt(var + LN_EPS) * ln_g.astype(jnp.float32) + ln_b.astype(jnp.float32)
    nc = s // CHUNK
    vc = vn.reshape(b, nc, CHUNK, N_CH_GROUPS, CH_GROUP)
    causal = jnp.tril(jnp.ones((CHUNK, CHUNK), dtype=bool))
    wm = jnp.where(causal[None], w_s.astype(jnp.float32), 0.0)
    sv = jnp.einsum('gts,bnsgc->bntgc', wm, vc) + b_s.astype(jnp.float32).T[None, None, :, :, None]
    sgu = u.astype(jnp.float32) * sv.reshape(b, s, MIX_WIDTH)
    y = sgu.astype(x.dtype) * jax.nn.silu(z)
    return x + y @ w_out


def setup_inputs(seed: int = 0) -> dict:
    key = jax.random.key(seed)
    ks = jax.random.split(key, 14)
    f32 = jnp.float32
    x = jax.random.normal(ks[0], (BATCH, SEQ, D_MODEL), f32)
    positions = jnp.broadcast_to(jnp.arange(SEQ, dtype=jnp.int32)[None, :], (BATCH, SEQ))
    a_norm_g = 1.0 + 0.01 * jax.random.normal(ks[1], (N_A, D_MODEL), f32)
    a_w_in = jax.random.normal(ks[2], (N_A, D_MODEL, A_IN), f32) * D_MODEL ** -0.5
    a_w_out = jax.random.normal(ks[3], (N_A, MIX_WIDTH, D_MODEL), f32) * MIX_WIDTH ** -0.5
    b_norm_g = 1.0 + 0.01 * jax.random.normal(ks[4], (N_B, D_MODEL), f32)
    b_w_in = jax.random.normal(ks[5], (N_B, D_MODEL, B_IN), f32) * D_MODEL ** -0.5
    b_ln_g = 1.0 + 0.01 * jax.random.normal(ks[6], (N_B, MIX_WIDTH), f32)
    b_ln_b = 0.01 * jax.random.normal(ks[7], (N_B, MIX_WIDTH), f32)
    b_w_s = jax.random.normal(ks[8], (N_B, N_CH_GROUPS, CHUNK, CHUNK), f32) * CHUNK ** -0.5
    b_b_s = 1.0 + 0.02 * jax.random.normal(ks[9], (N_B, N_CH_GROUPS, CHUNK), f32)
    b_w_out = jax.random.normal(ks[10], (N_B, MIX_WIDTH, D_MODEL), f32) * MIX_WIDTH ** -0.5
    final_norm_g = 1.0 + 0.01 * jax.random.normal(ks[11], (D_MODEL,), f32)
    return {"x": x, "positions": positions,
            "a_norm_g": a_norm_g, "a_w_in": a_w_in, "a_w_out": a_w_out,
            "b_norm_g": b_norm_g, "b_w_in": b_w_in, "b_ln_g": b_ln_g, "b_ln_b": b_ln_b,
            "b_w_s": b_w_s, "b_b_s": b_b_s, "b_w_out": b_w_out,
            "final_norm_g": final_norm_g}


def reference(x, positions, a_norm_g, a_w_in, a_w_out, b_norm_g, b_w_in, b_ln_g, b_ln_b,
              b_w_s, b_b_s, b_w_out, final_norm_g):
    for i in range(DEPTH):
        j = i // 2
        if i % 2 == 0:
            x = dilated_attention_layer(x, positions, a_norm_g[j], a_w_in[j], a_w_out[j])
        else:
            x = chunked_spatial_gating_layer(x, b_norm_g[j], b_w_in[j], b_ln_g[j], b_ln_b[j],
                                             b_w_s[j], b_b_s[j], b_w_out[j])
    return rmsnorm(x, final_norm_g)
```

```python
import functools

import numpy as np
import jax
import jax.numpy as jnp
from jax import lax
from jax.experimental import pallas as pl
from jax.experimental.pallas import tpu as pltpu

F32 = jnp.float32
BF16 = jnp.bfloat16

D_MODEL = 2048
N_HEADS = 16
HEAD_DIM = 128
MIX_WIDTH = N_HEADS * HEAD_DIM
DIL_GROUPS = ((128, 1), (512, 4), (2048, 16))
N_GROUPS = len(DIL_GROUPS)
ROPE_THETA = 500000.0
ROT_DIM = HEAD_DIM // 4
CHUNK = 128
N_CH_GROUPS = 16
CH_GROUP = MIX_WIDTH // N_CH_GROUPS
RMS_EPS = 1e-6
LN_EPS = 1e-5
NEG_INF = -1e30
A_IN = N_GROUPS * 3 * MIX_WIDTH + MIX_WIDTH
B_IN = 3 * MIX_WIDTH

LANES = 128
WINDOW_BLK = 128
VMEM_LIMIT = 56 * 1024 * 1024

PROJ_TM = 1024
PROJ_TN = 1024
OUT_TM = 512
ATTN_TL = 512


def _params(sem):
    return pltpu.CompilerParams(dimension_semantics=sem, vmem_limit_bytes=VMEM_LIMIT)


def _rope_table_kernel(pos_ref, invf_ref, cos_ref, sin_ref):
    ang = pos_ref[...].astype(F32) * invf_ref[...]
    lane = lax.broadcasted_iota(jnp.int32, ang.shape, 1)
    c = jnp.cos(ang)
    s = jnp.sin(ang)
    cos_k = jnp.where(lane < ROT_DIM, c, 1.0)
    sin_k = jnp.where(lane < ROT_DIM // 2, -s, jnp.where(lane < ROT_DIM, s, 0.0))
    scale = HEAD_DIM ** -0.5
    cos_ref[0] = cos_k * scale
    sin_ref[0] = sin_k * scale
    cos_ref[1] = cos_k
    sin_ref[1] = sin_k


def _rope_tables(positions):
    t = positions.size
    half = ROT_DIM // 2
    inv_freq = ROPE_THETA ** (-(jnp.arange(half, dtype=F32) * 2.0) / ROT_DIM)
    invf = jnp.zeros((1, LANES), F32).at[0, :ROT_DIM].set(jnp.tile(inv_freq, 2))
    tm = 1024
    return pl.pallas_call(
        _rope_table_kernel,
        grid=(t // tm,),
        in_specs=[pl.BlockSpec((tm, 1), lambda i: (i, 0)),
                  pl.BlockSpec((1, LANES), lambda i: (0, 0))],
        out_specs=[pl.BlockSpec((2, tm, LANES), lambda i: (0, i, 0)),
                   pl.BlockSpec((2, tm, LANES), lambda i: (0, i, 0))],
        out_shape=[jax.ShapeDtypeStruct((2, t, LANES), F32)] * 2,
        compiler_params=_params(("parallel",)),
        name="rope_tables",
    )(positions.reshape(t, 1), invf)


def _rmsnorm_kernel(x_ref, g_ref, o_ref):
    x = x_ref[...]
    ms = jnp.mean(x * x, axis=-1, keepdims=True)
    o_ref[...] = (x * lax.rsqrt(ms + RMS_EPS) * g_ref[...]).astype(o_ref.dtype)


def _rmsnorm(x, g):
    t, d = x.shape
    tm = 512
    return pl.pallas_call(
        _rmsnorm_kernel,
        grid=(t // tm,),
        in_specs=[pl.BlockSpec((tm, d), lambda i: (i, 0)),
                  pl.BlockSpec((1, d), lambda i: (0, 0))],
        out_specs=pl.BlockSpec((tm, d), lambda i: (i, 0)),
        out_shape=jax.ShapeDtypeStruct((t, d), BF16),
        compiler_params=_params(("parallel",)),
        name="rmsnorm_first",
    )(x, g.reshape(1, d))


def _gelu(x):
    return 0.5 * x * (1.0 + lax.erf(x * np.float32(np.sqrt(0.5))))


def _silu(x):
    return x * jax.nn.sigmoid(x)


def _inproj_kernel(*refs, epilogue):
    if epilogue == "rope":
        hn_ref, w_ref, cos_ref, sin_ref, o_ref, wb_ref, acc_ref = refs
    else:
        hn_ref, w_ref, o_ref, wb_ref = refs

    @pl.when(pl.program_id(1) == 0)
    def _():
        wb_ref[...] = w_ref[...].astype(BF16)

    acc = jnp.dot(hn_ref[...], wb_ref[...], preferred_element_type=F32)
    if epilogue == "plain":
        o_ref[...] = acc.astype(o_ref.dtype)
    elif epilogue == "silu":
        o_ref[...] = _silu(acc).astype(o_ref.dtype)
    elif epilogue == "gelu":
        o_ref[...] = _gelu(acc).astype(o_ref.dtype)
    else:
        acc_ref[...] = acc
        cos = cos_ref[...]
        sin = sin_ref[...]
        lane = lax.broadcasted_iota(jnp.int32, cos.shape, 1)
        first_half = lane < ROT_DIM // 2
        for h in range(acc_ref.shape[1] // HEAD_DIM):
            hs = slice(h * HEAD_DIM, (h + 1) * HEAD_DIM)
            t = acc_ref[:, hs]
            partner = jnp.where(first_half,
                                pltpu.roll(t, HEAD_DIM - ROT_DIM // 2, 1),
                                pltpu.roll(t, ROT_DIM // 2, 1))
            o_ref[:, hs] = (t * cos + partner * sin).astype(o_ref.dtype)


def _inproj(hn, w, layer, col_block, n_col_blocks, epilogue, name, tables=None, table_sel=None):
    t, k = hn.shape
    tm, tn = PROJ_TM, PROJ_TN
    in_specs = [pl.BlockSpec((tm, k), lambda j, i: (i, 0)),
                pl.BlockSpec((None, k, tn), lambda j, i: (layer, 0, col_block(j)))]
    args = [hn, w]
    scratch = [pltpu.VMEM((k, tn), BF16)]
    if epilogue == "rope":
        cos_t, sin_t = tables
        tspec = pl.BlockSpec((None, tm, LANES), lambda j, i: (table_sel(j), i, 0))
        in_specs += [tspec, tspec]
        args += [cos_t, sin_t]
        scratch.append(pltpu.VMEM((tm, tn), F32))
    return pl.pallas_call(
        functools.partial(_inproj_kernel, epilogue=epilogue),
        grid=(n_col_blocks, t // tm),
        in_specs=in_specs,
        out_specs=pl.BlockSpec((tm, tn), lambda j, i: (i, j)),
        out_shape=jax.ShapeDtypeStruct((t, n_col_blocks * tn), BF16),
        scratch_shapes=scratch,
        compiler_params=_params(("arbitrary", "arbitrary")),
        name=name,
    )(*args)


def _a_inproj(hn, w_in, layer, tables):
    bpm = MIX_WIDTH // PROJ_TN
    qk = _inproj(hn, w_in, layer, lambda j: (j // (2 * bpm)) * (3 * bpm) + j % (2 * bpm),
                 N_GROUPS * 2 * bpm, "rope", "a_inproj_qk", tables,
                 lambda j: (j % (2 * bpm)) // bpm)
    v = _inproj(hn, w_in, layer, lambda j: (j // bpm) * (3 * bpm) + 2 * bpm + j % bpm,
                N_GROUPS * bpm, "plain", "a_inproj_v")
    gate = _inproj(hn, w_in, layer, lambda j: N_GROUPS * 3 * bpm + j, bpm, "silu",
                   "a_inproj_gate")
    return qk, v, gate


def _b_inproj(hn, w_in, layer):
    bpm = MIX_WIDTH // PROJ_TN
    uv = _inproj(hn, w_in, layer, lambda j: j, 2 * bpm, "gelu", "b_inproj_uv")
    z = _inproj(hn, w_in, layer, lambda j: 2 * bpm + j, bpm, "silu", "b_inproj_z")
    return uv, z


def _attn_kernel(*refs, nsub, has_prev, merge, final):
    it = iter(refs)
    q_ref, k_ref, v_ref = next(it), next(it), next(it)
    kp_ref = vp_ref = op_ref = lp_ref = gate_ref = lse_ref = None
    if has_prev:
        kp_ref, vp_ref = next(it), next(it)
    if merge:
        op_ref, lp_ref = next(it), next(it)
    if final:
        gate_ref = next(it)
    o_ref = next(it)
    if not final:
        lse_ref = next(it)

    blk = WINDOW_BLK
    first_step = pl.program_id(2) == 0
    qi2 = lax.broadcasted_iota(jnp.int32, (blk, 2 * blk), 0)
    kj2 = lax.broadcasted_iota(jnp.int32, (blk, 2 * blk), 1)
    band = (kj2 >= qi2) & (kj2 <= qi2 + blk)
    qi1 = lax.broadcasted_iota(jnp.int32, (blk, blk), 0)
    kj1 = lax.broadcasted_iota(jnp.int32, (blk, blk), 1)
    causal = kj1 <= qi1
    lane = kj1

    for c in range(nsub):
        rows = slice(c * blk, (c + 1) * blk)
        lse_tile = jnp.zeros((blk, LANES), F32)
        for h in range(N_HEADS):
            hs = slice(h * HEAD_DIM, (h + 1) * HEAD_DIM)
            qh = q_ref[rows, hs]
            if c == 0 and not has_prev:
                kk, vv, mask = k_ref[rows, hs], v_ref[rows, hs], causal
            elif c == 0:
                kk = jnp.concatenate([kp_ref[:, hs], k_ref[rows, hs]], axis=0)
                vv = jnp.concatenate([vp_ref[:, hs], v_ref[rows, hs]], axis=0)
                mask = band & ((kj2 >= blk) | jnp.logical_not(first_step))
            else:
                krows = slice((c - 1) * blk, (c + 1) * blk)
                kk, vv, mask = k_ref[krows, hs], v_ref[krows, hs], band
            s = lax.dot_general(qh, kk, (((1,), (1,)), ((), ())), preferred_element_type=F32)
            s = jnp.where(mask, s, NEG_INF)
            m = jnp.max(s, axis=-1, keepdims=True)
            p = jnp.exp(s - m)
            l = jnp.sum(p, axis=-1, keepdims=True)
            acc = jnp.dot(p.astype(BF16), vv, preferred_element_type=F32)
            o = acc * (1.0 / l)
            lse = m + jnp.log(l)
            if merge:
                lp = lp_ref[rows, h:h + 1]
                mx = jnp.maximum(lp, lse)
                a = jnp.exp(lp - mx)
                b = jnp.exp(lse - mx)
                den = a + b
                inv = 1.0 / den
                o = (a * inv) * op_ref[rows, hs].astype(F32) + (b * inv) * o
                lse = mx + jnp.log(den)
            if final:
                o_ref[rows, hs] = (o * gate_ref[rows, hs].astype(F32)).astype(o_ref.dtype)
            else:
                o_ref[rows, hs] = o.astype(o_ref.dtype)
                lse_tile = jnp.where(lane == h, lse, lse_tile)
        if not final:
            lse_ref[rows, :] = lse_tile


def _attn_group(g, qk, v, batch, seq, prev=None, gate=None):
    _, d = DIL_GROUPS[g]
    res_len = seq // d
    tl = min(res_len, ATTN_TL)
    nq = res_len // tl
    has_prev = nq > 1
    merge = prev is not None
    final = gate is not None
    w = MIX_WIDTH
    qk3 = qk.reshape(batch, res_len, d * N_GROUPS * 2 * w)
    v3 = v.reshape(batch, res_len, d * N_GROUPS * w)
    nqk, nv = N_GROUPS * 2, N_GROUPS
    sub = tl // WINDOW_BLK

    in_specs = [pl.BlockSpec((None, tl, w), lambda b, r, n: (b, n, r * nqk + 2 * g)),
                pl.BlockSpec((None, tl, w), lambda b, r, n: (b, n, r * nqk + 2 * g + 1)),
                pl.BlockSpec((None, tl, w), lambda b, r, n: (b, n, r * nv + g))]
    args = [qk3, qk3, v3]
    if has_prev:
        prev_blk = lambda n: jnp.maximum(n * sub - 1, 0)
        in_specs += [pl.BlockSpec((None, WINDOW_BLK, w),
                                  lambda b, r, n: (b, prev_blk(n), r * nqk + 2 * g + 1)),
                     pl.BlockSpec((None, WINDOW_BLK, w),
                                  lambda b, r, n: (b, prev_blk(n), r * nv + g))]
        args += [qk3, v3]
    o_spec = pl.BlockSpec((None, tl, w), lambda b, r, n: (b, n, r))
    lse_spec = pl.BlockSpec((None, tl, LANES), lambda b, r, n: (b, n, r))
    if merge:
        o_prev, lse_prev = prev
        in_specs += [o_spec, lse_spec]
        args += [o_prev.reshape(batch, res_len, d * w), lse_prev.reshape(batch, res_len, d * LANES)]
    if final:
        in_specs.append(o_spec)
        args.append(gate.reshape(batch, res_len, d * w))
    out_specs = [o_spec]
    out_shape = [jax.ShapeDtypeStruct((batch, res_len, d * w), BF16)]
    if not final:
        out_specs.append(lse_spec)
        out_shape.append(jax.ShapeDtypeStruct((batch, res_len, d * LANES), F32))
    outs = pl.pallas_call(
        functools.partial(_attn_kernel, nsub=sub, has_prev=has_prev, merge=merge, final=final),
        grid=(batch, d, nq),
        in_specs=in_specs,
        out_specs=out_specs,
        out_shape=out_shape,
        compiler_params=_params(("parallel", "parallel", "arbitrary")),
        name=f"attn_group{g}",
    )(*args)
    t = batch * seq
    if final:
        return outs[0].reshape(t, w)
    return outs[0].reshape(t, w), outs[1].reshape(t, LANES)


def _residual_norm(out, g_ref, xo_ref, hn_ref):
    ms = jnp.mean(out * out, axis=-1, keepdims=True)
    nrm = out * lax.rsqrt(ms + RMS_EPS) * g_ref[...]
    if hn_ref is None:
        xo_ref[...] = nrm
    else:
        xo_ref[...] = out
        hn_ref[...] = nrm.astype(hn_ref.dtype)


def _a_out_kernel(y_ref, x_ref, w_ref, g_ref, xo_ref, hn_ref):
    out = x_ref[...] + jnp.dot(y_ref[...], w_ref[...], preferred_element_type=F32)
    _residual_norm(out, g_ref, xo_ref, hn_ref)


def _b_out_kernel(u_ref, v_ref, z_ref, x_ref, lng_ref, lnb_ref, ws_ref, bst_ref, w_ref, g_ref,
                  *rest, final):
    if final:
        xo_ref, vn_ref, y_ref = rest
        hn_ref = None
    else:
        xo_ref, hn_ref, vn_ref, y_ref = rest
    vf = v_ref[...].astype(F32)
    mu = jnp.mean(vf, axis=-1, keepdims=True)
    xc = vf - mu
    var = jnp.mean(xc * xc, axis=-1, keepdims=True)
    vn_ref[...] = (xc * lax.rsqrt(var + LN_EPS) * lng_ref[...] + lnb_ref[...]).astype(vn_ref.dtype)
    row = lax.broadcasted_iota(jnp.int32, (CHUNK, CHUNK), 0)
    col = lax.broadcasted_iota(jnp.int32, (CHUNK, CHUNK), 1)
    tril = row >= col
    for g in range(N_CH_GROUPS):
        gs = slice(g * CH_GROUP, (g + 1) * CH_GROUP)
        wm = jnp.where(tril, ws_ref[g], 0.0).astype(BF16)
        bias = bst_ref[:, g:g + 1]
        for n in range(u_ref.shape[0] // CHUNK):
            rows = slice(n * CHUNK, (n + 1) * CHUNK)
            sv = jnp.dot(wm, vn_ref[rows, gs], preferred_element_type=F32) + bias
            sgu = u_ref[rows, gs].astype(F32) * sv
            y_ref[rows, gs] = (sgu * z_ref[rows, gs].astype(F32)).astype(y_ref.dtype)
    out = x_ref[...] + jnp.dot(y_ref[...], w_ref[...], preferred_element_type=F32)
    _residual_norm(out, g_ref, xo_ref, hn_ref)


def _out_specs_shapes(t, d, tm, final):
    row_spec = pl.BlockSpec((tm, d), lambda i: (i, 0))
    if final:
        return [row_spec], [jax.ShapeDtypeStruct((t, d), F32)]
    return [row_spec, row_spec], [jax.ShapeDtypeStruct((t, d), F32), jax.ShapeDtypeStruct((t, d), BF16)]


def _a_out(y, x, w_out, g_next):
    t, d = x.shape
    tm = OUT_TM
    row_spec = pl.BlockSpec((tm, d), lambda i: (i, 0))
    out_specs, out_shape = _out_specs_shapes(t, d, tm, False)
    return pl.pallas_call(
        _a_out_kernel,
        grid=(t // tm,),
        in_specs=[row_spec, row_spec,
                  pl.BlockSpec((d, d), lambda i: (0, 0)),
                  pl.BlockSpec((1, d), lambda i: (0, 0))],
        out_specs=out_specs,
        out_shape=out_shape,
        compiler_params=_params(("parallel",)),
        name="a_outproj",
    )(y, x, w_out.astype(BF16), g_next.reshape(1, d))


def _b_out(uv, z, x, ln_g, ln_b, w_s, b_s, w_out, g_next, final):
    t, d = x.shape
    tm = OUT_TM
    row_spec = pl.BlockSpec((tm, d), lambda i: (i, 0))
    vec_spec = pl.BlockSpec((1, d), lambda i: (0, 0))
    out_specs, out_shape = _out_specs_shapes(t, d, tm, final)
    outs = pl.pallas_call(
        functools.partial(_b_out_kernel, final=final),
        grid=(t // tm,),
        in_specs=[pl.BlockSpec((tm, d), lambda i: (i, 0)),
                  pl.BlockSpec((tm, d), lambda i: (i, 1)),
                  row_spec, row_spec, vec_spec, vec_spec,
                  pl.BlockSpec((N_CH_GROUPS, CHUNK, CHUNK), lambda i: (0, 0, 0)),
                  pl.BlockSpec((CHUNK, N_CH_GROUPS), lambda i: (0, 0)),
                  pl.BlockSpec((d, d), lambda i: (0, 0)),
                  vec_spec],
        out_specs=out_specs,
        out_shape=out_shape,
        scratch_shapes=[pltpu.VMEM((tm, d), BF16), pltpu.VMEM((tm, d), BF16)],
        compiler_params=_params(("parallel",)),
        name="b_gate_outproj",
    )(uv, uv, z, x, ln_g.reshape(1, d), ln_b.reshape(1, d), w_s, b_s.T,
      w_out.astype(BF16), g_next.reshape(1, d))
    return outs


def kernel(x, positions, a_norm_g, a_w_in, a_w_out, b_norm_g, b_w_in, b_ln_g, b_ln_b,
           b_w_s, b_b_s, b_w_out, final_norm_g):
    batch, seq, d = x.shape
    depth = a_w_in.shape[0] + b_w_in.shape[0]
    xf = x.reshape(batch * seq, d)
    tables = _rope_tables(positions)
    hn = _rmsnorm(xf, a_norm_g[0])
    for layer in range(depth):
        j = layer // 2
        if layer % 2 == 0:
            qk, v, gate = _a_inproj(hn, a_w_in, j, tables)
            res = _attn_group(0, qk, v, batch, seq)
            res = _attn_group(1, qk, v, batch, seq, prev=res)
            y = _attn_group(2, qk, v, batch, seq, prev=res, gate=gate)
            xf, hn = _a_out(y, xf, a_w_out[j], b_norm_g[j])
        else:
            uv, z = _b_inproj(hn, b_w_in, j)
            final = layer == depth - 1
            g_next = final_norm_g if final else a_norm_g[j + 1]
            outs = _b_out(uv, z, xf, b_ln_g[j], b_ln_b[j], b_w_s[j], b_b_s[j], b_w_out[j],
                          g_next, final)
            if final:
                xf = outs[0]
            else:
                xf, hn = outs
    return xf.reshape(batch, seq, d)
```

```python
import functools

import numpy as np
import jax
import jax.numpy as jnp
from jax import lax
from jax.experimental import pallas as pl
from jax.experimental.pallas import tpu as pltpu

F32 = jnp.float32
BF16 = jnp.bfloat16

D_MODEL = 2048
N_HEADS = 16
HEAD_DIM = 128
MIX_WIDTH = N_HEADS * HEAD_DIM
DIL_GROUPS = ((128, 1), (512, 4), (2048, 16))
N_GROUPS = len(DIL_GROUPS)
ROPE_THETA = 500000.0
ROT_DIM = HEAD_DIM // 4
CHUNK = 128
N_CH_GROUPS = 16
CH_GROUP = MIX_WIDTH // N_CH_GROUPS
RMS_EPS = 1e-6
LN_EPS = 1e-5
NEG_INF = -1e30

LANES = 128
BLK = 128
VMEM_LIMIT = 56 * 1024 * 1024

PROJ_TM = 1024
PROJ_TN = 1024
OUT_TM = 512
HUB = 4
HEADS_PER_STEP = 2


def _params(sem):
    return pltpu.CompilerParams(dimension_semantics=sem, vmem_limit_bytes=VMEM_LIMIT)


def _strided(start, size, stride):
    return slice(None) if stride == 1 else pl.ds(start, size, stride=stride)


def _rope_table_kernel(pos_ref, invf_ref, cos_ref, sin_ref):
    ang = pos_ref[...].astype(F32) * invf_ref[...]
    lane = lax.broadcasted_iota(jnp.int32, ang.shape, 1)
    c = jnp.cos(ang)
    s = jnp.sin(ang)
    cos_k = jnp.where(lane < ROT_DIM, c, 1.0)
    sin_k = jnp.where(lane < ROT_DIM // 2, -s, jnp.where(lane < ROT_DIM, s, 0.0))
    scale = HEAD_DIM ** -0.5
    cos_ref[0] = cos_k * scale
    sin_ref[0] = sin_k * scale
    cos_ref[1] = cos_k
    sin_ref[1] = sin_k


def _rope_tables(positions):
    t = positions.size
    half = ROT_DIM // 2
    inv_freq = ROPE_THETA ** (-(jnp.arange(half, dtype=F32) * 2.0) / ROT_DIM)
    invf = jnp.zeros((1, LANES), F32).at[0, :ROT_DIM].set(jnp.tile(inv_freq, 2))
    tm = 1024
    return pl.pallas_call(
        _rope_table_kernel,
        grid=(t // tm,),
        in_specs=[pl.BlockSpec((tm, 1), lambda i: (i, 0)),
                  pl.BlockSpec((1, LANES), lambda i: (0, 0))],
        out_specs=[pl.BlockSpec((2, tm, LANES), lambda i: (0, i, 0)),
                   pl.BlockSpec((2, tm, LANES), lambda i: (0, i, 0))],
        out_shape=[jax.ShapeDtypeStruct((2, t, LANES), F32)] * 2,
        compiler_params=_params(("parallel",)),
        name="rope_tables",
    )(positions.reshape(t, 1), invf)


def _rmsnorm_kernel(x_ref, g_ref, o_ref):
    x = x_ref[...]
    ms = jnp.mean(x * x, axis=-1, keepdims=True)
    o_ref[...] = (x * lax.rsqrt(ms + RMS_EPS) * g_ref[...]).astype(o_ref.dtype)


def _rmsnorm(x, g):
    t, d = x.shape
    tm = 512
    return pl.pallas_call(
        _rmsnorm_kernel,
        grid=(t // tm,),
        in_specs=[pl.BlockSpec((tm, d), lambda i: (i, 0)),
                  pl.BlockSpec((1, d), lambda i: (0, 0))],
        out_specs=pl.BlockSpec((tm, d), lambda i: (i, 0)),
        out_shape=jax.ShapeDtypeStruct((t, d), BF16),
        compiler_params=_params(("parallel",)),
        name="rmsnorm_first",
    )(x, g.reshape(1, d))


def _gelu(x):
    return 0.5 * x * (1.0 + lax.erf(x * np.float32(np.sqrt(0.5))))


def _silu(x):
    return x * jax.nn.sigmoid(x)


_ACTIVATIONS = {"plain": lambda x: x, "silu": _silu, "gelu": _gelu}


def _inproj_kernel(*refs, epilogue, dil):
    if epilogue == "rope":
        hn_ref, w_ref, cos_ref, sin_ref, o_ref, wb_ref, acc_ref = refs
    elif dil > 1:
        hn_ref, w_ref, o_ref, wb_ref, acc_ref = refs
    else:
        hn_ref, w_ref, o_ref, wb_ref = refs

    @pl.when(pl.program_id(1) == 0)
    def _():
        wb_ref[...] = w_ref[...].astype(BF16)

    acc = jnp.dot(hn_ref[...], wb_ref[...], preferred_element_type=F32)
    tm, tn = acc.shape
    if epilogue != "rope" and dil == 1:
        o_ref[0] = _ACTIVATIONS[epilogue](acc).astype(o_ref.dtype)
        return

    n_slabs = tn // LANES
    for c in range(n_slabs):
        acc_ref[c] = acc[:, c * LANES:(c + 1) * LANES]
    lane = lax.broadcasted_iota(jnp.int32, (tm // dil, LANES), 1)
    first_half = lane < ROT_DIM // 2
    for r in range(dil):
        rows = _strided(r, tm // dil, dil)
        if epilogue == "rope":
            cos = cos_ref[rows, :]
            sin = sin_ref[rows, :]
        for c in range(n_slabs):
            t = acc_ref[c, rows, :]
            if epilogue == "rope":
                partner = jnp.where(first_half,
                                    pltpu.roll(t, HEAD_DIM - ROT_DIM // 2, 1),
                                    pltpu.roll(t, ROT_DIM // 2, 1))
                val = t * cos + partner * sin
            else:
                val = _ACTIVATIONS[epilogue](t)
            o_ref[r, :, c * LANES:(c + 1) * LANES] = val.astype(o_ref.dtype)


def _inproj(hn, w, layer, batch, col_block, n_col_blocks, epilogue, dil, name,
            tables=None, table_sel=None):
    t, k = hn.shape
    seq = t // batch
    tm, tn = PROJ_TM, PROJ_TN
    tiles_per_seq = seq // tm
    in_specs = [pl.BlockSpec((tm, k), lambda j, i: (i, 0)),
                pl.BlockSpec((None, k, tn), lambda j, i: (layer, 0, col_block(j)))]
    args = [hn, w]
    scratch = [pltpu.VMEM((k, tn), BF16)]
    if epilogue == "rope":
        cos_t, sin_t = tables
        tspec = pl.BlockSpec((None, tm, LANES), lambda j, i: (table_sel(j), i, 0))
        in_specs += [tspec, tspec]
        args += [cos_t, sin_t]
    if epilogue == "rope" or dil > 1:
        scratch.append(pltpu.VMEM((tn // LANES, tm, LANES), F32))
    return pl.pallas_call(
        functools.partial(_inproj_kernel, epilogue=epilogue, dil=dil),
        grid=(n_col_blocks, t // tm),
        in_specs=in_specs,
        out_specs=pl.BlockSpec((None, dil, tm // dil, tn),
                               lambda j, i: (i // tiles_per_seq, 0, i % tiles_per_seq, j)),
        out_shape=jax.ShapeDtypeStruct((batch, dil, seq // dil, n_col_blocks * tn), BF16),
        scratch_shapes=scratch,
        compiler_params=_params(("arbitrary", "arbitrary")),
        name=name,
    )(*args)


def _a_inproj(hn, w_in, layer, batch, tables):
    bpm = MIX_WIDTH // PROJ_TN
    d_last = DIL_GROUPS[2][1]
    qk01 = _inproj(hn, w_in, layer, batch,
                   lambda j: (j // (2 * bpm)) * (3 * bpm) + j % (2 * bpm), 4 * bpm,
                   "rope", HUB, "a_inproj_qk01", tables, lambda j: (j % (2 * bpm)) // bpm)
    v01 = _inproj(hn, w_in, layer, batch,
                  lambda j: (j // bpm) * (3 * bpm) + 2 * bpm + j % bpm, 2 * bpm,
                  "plain", HUB, "a_inproj_v01")
    qk2 = _inproj(hn, w_in, layer, batch, lambda j: 6 * bpm + j, 2 * bpm,
                  "rope", d_last, "a_inproj_qk2", tables, lambda j: j // bpm)
    v2 = _inproj(hn, w_in, layer, batch, lambda j: 8 * bpm + j, bpm,
                 "plain", d_last, "a_inproj_v2")
    gate = _inproj(hn, w_in, layer, batch, lambda j: 9 * bpm + j, bpm,
                   "silu", HUB, "a_inproj_gate")
    return qk01, v01, qk2, v2, gate


def _b_inproj(hn, w_in, layer, batch):
    bpm = MIX_WIDTH // PROJ_TN
    t = hn.shape[0]
    uv = _inproj(hn, w_in, layer, batch, lambda j: j, 2 * bpm, "gelu", 1, "b_inproj_uv")
    z = _inproj(hn, w_in, layer, batch, lambda j: 2 * bpm + j, bpm, "silu", 1, "b_inproj_z")
    return uv.reshape(t, 2 * MIX_WIDTH), z.reshape(t, MIX_WIDTH)


def _softmax_block(q, k, v, mask, prev):
    s = lax.dot_general(q, k, (((1,), (1,)), ((), ())), preferred_element_type=F32)
    s = jnp.where(mask, s, NEG_INF)
    m_own = jnp.max(s, axis=-1, keepdims=True)
    if prev is None:
        m_new = jnp.broadcast_to(m_own, (BLK, LANES))
    else:
        m_prev, l_prev, acc_prev = prev
        m_new = jnp.maximum(m_prev, m_own)
    p = jnp.exp(s - jnp.concatenate([m_new] * (s.shape[1] // LANES), axis=1))
    l_own = jnp.sum(p, axis=-1, keepdims=True)
    pv = jnp.dot(p.astype(BF16), v, preferred_element_type=F32)
    if prev is None:
        return m_new, jnp.broadcast_to(l_own, (BLK, LANES)), pv
    alpha = jnp.exp(m_prev - m_new)
    return m_new, alpha * l_prev + l_own, alpha * acc_prev + pv


def _attn_kernel(q0_ref, k0_ref, v0_ref, q1_ref, k1_ref, v1_ref, q2_ref, k2_ref, v2_ref,
                 gate_ref, y_ref, m_ref, l_ref, acc_ref):
    seq = HUB * q0_ref.shape[1]
    piece = BLK // HUB
    n_blocks = seq // BLK

    qi = lax.broadcasted_iota(jnp.int32, (BLK, 2 * BLK), 0)
    kc = lax.broadcasted_iota(jnp.int32, (BLK, 2 * BLK), 1)
    tq = HUB * (qi % piece) + qi // piece
    tk = HUB * (kc % (2 * piece) - piece) + kc // (2 * piece)
    mask0 = (tq - tk >= 0) & (tq - tk <= BLK)
    qi1 = lax.broadcasted_iota(jnp.int32, (BLK, BLK), 0)
    kc1 = lax.broadcasted_iota(jnp.int32, (BLK, BLK), 1)
    mask0_first = HUB * (kc1 % piece) + kc1 // piece <= HUB * (qi1 % piece) + qi1 // piece
    band = (kc >= qi) & (kc <= qi + BLK)
    causal = kc1 <= qi1

    def gather4(ref, lo, hi, hs):
        return jnp.concatenate([ref[r, lo:hi, hs] for r in range(HUB)], axis=0)

    for h in range(HEADS_PER_STEP):
        hs = slice(h * HEAD_DIM, (h + 1) * HEAD_DIM)
        for a in range(n_blocks):
            lo, hi = a * piece, (a + 1) * piece
            q = gather4(q0_ref, lo, hi, hs)
            if a == 0:
                k, v, mask = gather4(k0_ref, lo, hi, hs), gather4(v0_ref, lo, hi, hs), mask0_first
            else:
                k, v, mask = (gather4(k0_ref, lo - piece, hi, hs),
                              gather4(v0_ref, lo - piece, hi, hs), mask0)
            m, l, acc = _softmax_block(q, k, v, mask, None)
            for r in range(HUB):
                rs = slice(r * piece, (r + 1) * piece)
                m_ref[h, r, lo:hi, :] = m[rs]
                l_ref[h, r, lo:hi, :] = l[rs]
                acc_ref[h, r, lo:hi, :] = acc[rs]
        for r in range(HUB):
            for n in range(seq // HUB // BLK):
                rows = slice(n * BLK, (n + 1) * BLK)
                q = q1_ref[r, rows, hs]
                if n == 0:
                    k, v, mask = k1_ref[r, rows, hs], v1_ref[r, rows, hs], causal
                else:
                    krows = slice((n - 1) * BLK, (n + 1) * BLK)
                    k, v, mask = k1_ref[r, krows, hs], v1_ref[r, krows, hs], band
                prev = (m_ref[h, r, rows, :], l_ref[h, r, rows, :], acc_ref[h, r, rows, :])
                m, l, acc = _softmax_block(q, k, v, mask, prev)
                m_ref[h, r, rows, :] = m
                l_ref[h, r, rows, :] = l
                acc_ref[h, r, rows, :] = acc
        for r16 in range(q2_ref.shape[0]):
            r, s = r16 % HUB, r16 // HUB
            rows = pl.ds(s, BLK, stride=HUB)
            prev = (m_ref[h, r, rows, :], l_ref[h, r, rows, :], acc_ref[h, r, rows, :])
            m, l, acc = _softmax_block(q2_ref[r16, :, hs], k2_ref[r16, :, hs], v2_ref[r16, :, hs],
                                       causal, prev)
            acc_ref[h, r, rows, :] = acc * (1.0 / l)
        for r in range(HUB):
            y_ref[r, :, hs] = (acc_ref[h, r] * gate_ref[r, :, hs].astype(F32)).astype(y_ref.dtype)


def _attention(qk01, v01, qk2, v2, gate):
    batch, hub, res_len, _ = gate.shape
    d2, len2 = qk2.shape[1], qk2.shape[2]
    wc = HEADS_PER_STEP * HEAD_DIM
    cpm = MIX_WIDTH // wc
    spec4 = lambda slab: pl.BlockSpec((None, hub, res_len, wc),
                                      lambda b, hp: (b, 0, 0, slab * cpm + hp))
    spec16 = lambda slab: pl.BlockSpec((None, d2, len2, wc),
                                       lambda b, hp: (b, 0, 0, slab * cpm + hp))
    state = pltpu.VMEM((HEADS_PER_STEP, hub, res_len, LANES), F32)
    return pl.pallas_call(
        _attn_kernel,
        grid=(batch, cpm),
        in_specs=[spec4(0), spec4(1), spec4(0),
                  spec4(2), spec4(3), spec4(1),
                  spec16(0), spec16(1), spec16(0),
                  spec4(0)],
        out_specs=spec4(0),
        out_shape=jax.ShapeDtypeStruct((batch, hub, res_len, MIX_WIDTH), BF16),
        scratch_shapes=[state, state, state],
        compiler_params=_params(("parallel", "parallel")),
        name="dilated_attention",
    )(qk01, qk01, v01, qk01, qk01, v01, qk2, qk2, v2, gate)


def _residual_norm(out, g_ref, xo_ref, hn_ref):
    ms = jnp.mean(out * out, axis=-1, keepdims=True)
    nrm = out * lax.rsqrt(ms + RMS_EPS) * g_ref[...]
    if hn_ref is None:
        xo_ref[...] = nrm
    else:
        xo_ref[...] = out
        hn_ref[...] = nrm.astype(hn_ref.dtype)


def _a_out_kernel(y_ref, x_ref, w_ref, g_ref, xo_ref, hn_ref, stage_ref):
    hub, rows_per, d = y_ref.shape
    tm = hub * rows_per
    res = jnp.dot(y_ref[...].reshape(tm, d), w_ref[...], preferred_element_type=F32)
    for c in range(d // LANES):
        for r in range(hub):
            stage_ref[c, pl.ds(r, rows_per, stride=hub), :] = (
                res[r * rows_per:(r + 1) * rows_per, c * LANES:(c + 1) * LANES])
    sq = jnp.zeros((tm, LANES), F32)
    for c in range(d // LANES):
        cs = slice(c * LANES, (c + 1) * LANES)
        out = x_ref[:, cs] + stage_ref[c]
        xo_ref[:, cs] = out
        sq = sq + out * out
    inv = lax.rsqrt(jnp.sum(sq, axis=-1, keepdims=True) * (1.0 / d) + RMS_EPS)
    hn_ref[...] = (xo_ref[...] * inv * g_ref[...]).astype(hn_ref.dtype)


def _b_out_kernel(u_ref, v_ref, z_ref, x_ref, lng_ref, lnb_ref, ws_ref, bst_ref, w_ref, g_ref,
                  *rest, final):
    if final:
        xo_ref, vn_ref, y_ref = rest
        hn_ref = None
    else:
        xo_ref, hn_ref, vn_ref, y_ref = rest
    vf = v_ref[...].astype(F32)
    mu = jnp.mean(vf, axis=-1, keepdims=True)
    xc = vf - mu
    var = jnp.mean(xc * xc, axis=-1, keepdims=True)
    vn_ref[...] = (xc * lax.rsqrt(var + LN_EPS) * lng_ref[...] + lnb_ref[...]).astype(vn_ref.dtype)
    row = lax.broadcasted_iota(jnp.int32, (CHUNK, CHUNK), 0)
    col = lax.broadcasted_iota(jnp.int32, (CHUNK, CHUNK), 1)
    tril = row >= col
    for g in range(N_CH_GROUPS):
        gs = slice(g * CH_GROUP, (g + 1) * CH_GROUP)
        wm = jnp.where(tril, ws_ref[g], 0.0).astype(BF16)
        bias = bst_ref[:, g:g + 1]
        for n in range(u_ref.shape[0] // CHUNK):
            rows = slice(n * CHUNK, (n + 1) * CHUNK)
            sv = jnp.dot(wm, vn_ref[rows, gs], preferred_element_type=F32) + bias
            sgu = u_ref[rows, gs].astype(F32) * sv
            y_ref[rows, gs] = (sgu * z_ref[rows, gs].astype(F32)).astype(y_ref.dtype)
    out = x_ref[...] + jnp.dot(y_ref[...], w_ref[...], preferred_element_type=F32)
    _residual_norm(out, g_ref, xo_ref, hn_ref)


def _a_out(y4, x, w_out, g_next):
    t, d = x.shape
    batch, hub, res_len, _ = y4.shape
    tm = OUT_TM
    tiles_per_seq = hub * res_len // tm
    row_spec = pl.BlockSpec((tm, d), lambda i: (i, 0))
    return pl.pallas_call(
        _a_out_kernel,
        grid=(t // tm,),
        in_specs=[pl.BlockSpec((None, hub, tm // hub, d),
                               lambda i: (i // tiles_per_seq, 0, i % tiles_per_seq, 0)),
                  row_spec,
                  pl.BlockSpec((d, d), lambda i: (0, 0)),
                  pl.BlockSpec((1, d), lambda i: (0, 0))],
        out_specs=[row_spec, row_spec],
        out_shape=[jax.ShapeDtypeStruct((t, d), F32), jax.ShapeDtypeStruct((t, d), BF16)],
        scratch_shapes=[pltpu.VMEM((d // LANES, tm, LANES), F32)],
        compiler_params=_params(("parallel",)),
        name="a_outproj",
    )(y4, x, w_out.astype(BF16), g_next.reshape(1, d))


def _b_out(uv, z, x, ln_g, ln_b, w_s, b_s, w_out, g_next, final):
    t, d = x.shape
    tm = OUT_TM
    row_spec = pl.BlockSpec((tm, d), lambda i: (i, 0))
    vec_spec = pl.BlockSpec((1, d), lambda i: (0, 0))
    if final:
        out_specs, out_shape = [row_spec], [jax.ShapeDtypeStruct((t, d), F32)]
    else:
        out_specs = [row_spec, row_spec]
        out_shape = [jax.ShapeDtypeStruct((t, d), F32), jax.ShapeDtypeStruct((t, d), BF16)]
    return pl.pallas_call(
        functools.partial(_b_out_kernel, final=final),
        grid=(t // tm,),
        in_specs=[pl.BlockSpec((tm, d), lambda i: (i, 0)),
                  pl.BlockSpec((tm, d), lambda i: (i, 1)),
                  row_spec, row_spec, vec_spec, vec_spec,
                  pl.BlockSpec((N_CH_GROUPS, CHUNK, CHUNK), lambda i: (0, 0, 0)),
                  pl.BlockSpec((CHUNK, N_CH_GROUPS), lambda i: (0, 0)),
                  pl.BlockSpec((d, d), lambda i: (0, 0)),
                  vec_spec],
        out_specs=out_specs,
        out_shape=out_shape,
        scratch_shapes=[pltpu.VMEM((tm, d), BF16), pltpu.VMEM((tm, d), BF16)],
        compiler_params=_params(("parallel",)),
        name="b_gate_outproj",
    )(uv, uv, z, x, ln_g.reshape(1, d), ln_b.reshape(1, d), w_s, b_s.T,
      w_out.astype(BF16), g_next.reshape(1, d))


def kernel(x, positions, a_norm_g, a_w_in, a_w_out, b_norm_g, b_w_in, b_ln_g, b_ln_b,
           b_w_s, b_b_s, b_w_out, final_norm_g):
    batch, seq, d = x.shape
    depth = a_w_in.shape[0] + b_w_in.shape[0]
    xf = x.reshape(batch * seq, d)
    tables = _rope_tables(positions)
    hn = _rmsnorm(xf, a_norm_g[0])
    for layer in range(depth):
        j = layer // 2
        if layer % 2 == 0:
            y4 = _attention(*_a_inproj(hn, a_w_in, j, batch, tables))
            xf, hn = _a_out(y4, xf, a_w_out[j], b_norm_g[j])
        else:
            uv, z = _b_inproj(hn, b_w_in, j, batch)
            final = layer == depth - 1
            g_next = final_norm_g if final else a_norm_g[j + 1]
            outs = _b_out(uv, z, xf, b_ln_g[j], b_ln_b[j], b_w_s[j], b_b_s[j], b_w_out[j],
                          g_next, final)
            if final:
                xf = outs[0]
            else:
                xf, hn = outs
    return xf.reshape(batch, seq, d)
```

```python
import functools

import numpy as np
import jax
import jax.numpy as jnp
from jax import lax
from jax.experimental import pallas as pl
from jax.experimental.pallas import tpu as pltpu

F32 = jnp.float32
BF16 = jnp.bfloat16

D_MODEL = 2048
N_HEADS = 16
HEAD_DIM = 128
MIX_WIDTH = N_HEADS * HEAD_DIM
DIL_GROUPS = ((128, 1), (512, 4), (2048, 16))
N_GROUPS = len(DIL_GROUPS)
ROPE_THETA = 500000.0
ROT_DIM = HEAD_DIM // 4
CHUNK = 128
N_CH_GROUPS = 16
CH_GROUP = MIX_WIDTH // N_CH_GROUPS
RMS_EPS = 1e-6
LN_EPS = 1e-5
NEG_INF = -1e30

LANES = 128
BLK = 128
VMEM_LIMIT = 56 * 1024 * 1024

PROJ_TM = 1024
PROJ_TN = 1024
OUT_TM = 512
HUB = 4
HEADS_PER_STEP = 2
SCORE_LOOKAHEAD = 4


def _params(sem):
    return pltpu.CompilerParams(dimension_semantics=sem, vmem_limit_bytes=VMEM_LIMIT)


def _strided(start, size, stride):
    return slice(None) if stride == 1 else pl.ds(start, size, stride=stride)


def _rope_table_kernel(pos_ref, invf_ref, cos_ref, sin_ref):
    ang = pos_ref[...].astype(F32) * invf_ref[...]
    lane = lax.broadcasted_iota(jnp.int32, ang.shape, 1)
    c = jnp.cos(ang)
    s = jnp.sin(ang)
    cos_k = jnp.where(lane < ROT_DIM, c, 1.0)
    sin_k = jnp.where(lane < ROT_DIM // 2, -s, jnp.where(lane < ROT_DIM, s, 0.0))
    scale = np.float32(HEAD_DIM ** -0.5 * np.log2(np.e))
    cos_ref[0] = cos_k * scale
    sin_ref[0] = sin_k * scale
    cos_ref[1] = cos_k
    sin_ref[1] = sin_k


def _rope_tables(positions):
    t = positions.size
    half = ROT_DIM // 2
    inv_freq = ROPE_THETA ** (-(jnp.arange(half, dtype=F32) * 2.0) / ROT_DIM)
    invf = jnp.zeros((1, LANES), F32).at[0, :ROT_DIM].set(jnp.tile(inv_freq, 2))
    tm = 1024
    return pl.pallas_call(
        _rope_table_kernel,
        grid=(t // tm,),
        in_specs=[pl.BlockSpec((tm, 1), lambda i: (i, 0)),
                  pl.BlockSpec((1, LANES), lambda i: (0, 0))],
        out_specs=[pl.BlockSpec((2, tm, LANES), lambda i: (0, i, 0)),
                   pl.BlockSpec((2, tm, LANES), lambda i: (0, i, 0))],
        out_shape=[jax.ShapeDtypeStruct((2, t, LANES), F32)] * 2,
        compiler_params=_params(("parallel",)),
        name="rope_tables",
    )(positions.reshape(t, 1), invf)


def _rmsnorm_kernel(x_ref, g_ref, o_ref):
    x = x_ref[...]
    ms = jnp.mean(x * x, axis=-1, keepdims=True)
    o_ref[...] = (x * lax.rsqrt(ms + RMS_EPS) * g_ref[...]).astype(o_ref.dtype)


def _rmsnorm(x, g):
    t, d = x.shape
    tm = 512
    return pl.pallas_call(
        _rmsnorm_kernel,
        grid=(t // tm,),
        in_specs=[pl.BlockSpec((tm, d), lambda i: (i, 0)),
                  pl.BlockSpec((1, d), lambda i: (0, 0))],
        out_specs=pl.BlockSpec((tm, d), lambda i: (i, 0)),
        out_shape=jax.ShapeDtypeStruct((t, d), BF16),
        compiler_params=_params(("parallel",)),
        name="rmsnorm_first",
    )(x, g.reshape(1, d))


def _gelu(x):
    return 0.5 * x * (1.0 + lax.erf(x * np.float32(np.sqrt(0.5))))


def _silu(x):
    return x * jax.nn.sigmoid(x)


_ACTIVATIONS = {"plain": lambda x: x, "silu": _silu, "gelu": _gelu}


def _inproj_kernel(*refs, epilogue, dil):
    if epilogue == "rope":
        hn_ref, w_ref, cos_ref, sin_ref, o_ref, wb_ref, acc_ref = refs
    elif dil > 1:
        hn_ref, w_ref, o_ref, wb_ref, acc_ref = refs
    else:
        hn_ref, w_ref, o_ref, wb_ref = refs

    @pl.when(pl.program_id(1) == 0)
    def _():
        wb_ref[...] = w_ref[...].astype(BF16)

    acc = jnp.dot(hn_ref[...], wb_ref[...], preferred_element_type=F32)
    tm, tn = acc.shape
    if epilogue != "rope" and dil == 1:
        o_ref[0] = _ACTIVATIONS[epilogue](acc).astype(o_ref.dtype)
        return

    n_slabs = tn // LANES
    for c in range(n_slabs):
        acc_ref[c] = acc[:, c * LANES:(c + 1) * LANES]
    lane = lax.broadcasted_iota(jnp.int32, (tm // dil, LANES), 1)
    first_half = lane < ROT_DIM // 2
    for r in range(dil):
        rows = _strided(r, tm // dil, dil)
        if epilogue == "rope":
            cos = cos_ref[rows, :]
            sin = sin_ref[rows, :]
        for c in range(n_slabs):
            t = acc_ref[c, rows, :]
            if epilogue == "rope":
                partner = jnp.where(first_half,
                                    pltpu.roll(t, HEAD_DIM - ROT_DIM // 2, 1),
                                    pltpu.roll(t, ROT_DIM // 2, 1))
                val = t * cos + partner * sin
            else:
                val = _ACTIVATIONS[epilogue](t)
            o_ref[r, :, c * LANES:(c + 1) * LANES] = val.astype(o_ref.dtype)


def _inproj(hn, w, layer, batch, col_block, n_col_blocks, epilogue, dil, name,
            tables=None, table_sel=None):
    t, k = hn.shape
    seq = t // batch
    tm, tn = PROJ_TM, PROJ_TN
    tiles_per_seq = seq // tm
    in_specs = [pl.BlockSpec((tm, k), lambda j, i: (i, 0)),
                pl.BlockSpec((None, k, tn), lambda j, i: (layer, 0, col_block(j)))]
    args = [hn, w]
    scratch = [pltpu.VMEM((k, tn), BF16)]
    if epilogue == "rope":
        cos_t, sin_t = tables
        tspec = pl.BlockSpec((None, tm, LANES), lambda j, i: (table_sel(j), i, 0))
        in_specs += [tspec, tspec]
        args += [cos_t, sin_t]
    if epilogue == "rope" or dil > 1:
        scratch.append(pltpu.VMEM((tn // LANES, tm, LANES), F32))
    return pl.pallas_call(
        functools.partial(_inproj_kernel, epilogue=epilogue, dil=dil),
        grid=(n_col_blocks, t // tm),
        in_specs=in_specs,
        out_specs=pl.BlockSpec((None, dil, tm // dil, tn),
                               lambda j, i: (i // tiles_per_seq, 0, i % tiles_per_seq, j)),
        out_shape=jax.ShapeDtypeStruct((batch, dil, seq // dil, n_col_blocks * tn), BF16),
        scratch_shapes=scratch,
        compiler_params=_params(("arbitrary", "arbitrary")),
        name=name,
    )(*args)


def _a_inproj(hn, w_in, layer, batch, tables):
    bpm = MIX_WIDTH // PROJ_TN
    d_last = DIL_GROUPS[2][1]
    qk01 = _inproj(hn, w_in, layer, batch,
                   lambda j: (j // (2 * bpm)) * (3 * bpm) + j % (2 * bpm), 4 * bpm,
                   "rope", HUB, "a_inproj_qk01", tables, lambda j: (j % (2 * bpm)) // bpm)
    v01 = _inproj(hn, w_in, layer, batch,
                  lambda j: (j // bpm) * (3 * bpm) + 2 * bpm + j % bpm, 2 * bpm,
                  "plain", HUB, "a_inproj_v01")
    qk2 = _inproj(hn, w_in, layer, batch, lambda j: 6 * bpm + j, 2 * bpm,
                  "rope", d_last, "a_inproj_qk2", tables, lambda j: j // bpm)
    v2 = _inproj(hn, w_in, layer, batch, lambda j: 8 * bpm + j, bpm,
                 "plain", d_last, "a_inproj_v2")
    gate = _inproj(hn, w_in, layer, batch, lambda j: 9 * bpm + j, bpm,
                   "silu", HUB, "a_inproj_gate")
    return qk01, v01, qk2, v2, gate


def _b_inproj(hn, w_in, layer, batch):
    bpm = MIX_WIDTH // PROJ_TN
    t = hn.shape[0]
    uv = _inproj(hn, w_in, layer, batch, lambda j: j, 2 * bpm, "gelu", 1, "b_inproj_uv")
    z = _inproj(hn, w_in, layer, batch, lambda j: 2 * bpm + j, bpm, "silu", 1, "b_inproj_z")
    return uv.reshape(t, 2 * MIX_WIDTH), z.reshape(t, MIX_WIDTH)


def _scores(q, k):
    return lax.dot_general(q, k, (((1,), (1,)), ((), ())), preferred_element_type=F32)


def _softmax_block(s, v, mask, prev):
    s = jnp.where(mask, s, NEG_INF)
    m_own = jnp.max(s, axis=-1, keepdims=True)
    if prev is None:
        m_new = jnp.broadcast_to(m_own, (BLK, LANES))
    else:
        m_prev, l_prev, acc_prev = prev
        m_new = jnp.maximum(m_prev, m_own)
    p = jnp.exp2(s - jnp.concatenate([m_new] * (s.shape[1] // LANES), axis=1))
    l_own = jnp.sum(p, axis=-1, keepdims=True)
    pv = jnp.dot(p.astype(BF16), v, preferred_element_type=F32)
    if prev is None:
        return m_new, jnp.broadcast_to(l_own, (BLK, LANES)), pv
    alpha = jnp.exp2(m_prev - m_new)
    return m_new, alpha * l_prev + l_own, alpha * acc_prev + pv


def _run_pipelined(jobs):
    ahead = SCORE_LOOKAHEAD
    scores = [fn() for fn, _ in jobs[:ahead]]
    for i, (_, finish) in enumerate(jobs):
        if i + ahead < len(jobs):
            scores.append(jobs[i + ahead][0]())
        finish(scores[i])
        scores[i] = None


def _attn_kernel(q0_ref, k0_ref, v0_ref, q1_ref, k1_ref, v1_ref, q2_ref, k2_ref, v2_ref,
                 gate_ref, y_ref, m_ref, l_ref, acc_ref, m16_ref, l16_ref, acc16_ref):
    seq = HUB * q0_ref.shape[1]
    piece = BLK // HUB
    heads = [slice(h * HEAD_DIM, (h + 1) * HEAD_DIM) for h in range(HEADS_PER_STEP)]
    state_refs = (m_ref, l_ref, acc_ref)
    state16_refs = (m16_ref, l16_ref, acc16_ref)

    qi = lax.broadcasted_iota(jnp.int32, (BLK, 2 * BLK), 0)
    kc = lax.broadcasted_iota(jnp.int32, (BLK, 2 * BLK), 1)
    tq = HUB * (qi % piece) + qi // piece
    tk = HUB * (kc % (2 * piece) - piece) + kc // (2 * piece)
    mask0 = (tq - tk >= 0) & (tq - tk <= BLK)
    qi1 = lax.broadcasted_iota(jnp.int32, (BLK, BLK), 0)
    kc1 = lax.broadcasted_iota(jnp.int32, (BLK, BLK), 1)
    mask0_first = HUB * (kc1 % piece) + kc1 // piece <= HUB * (qi1 % piece) + qi1 // piece
    band = (kc >= qi) & (kc <= qi + BLK)
    causal = kc1 <= qi1

    def gather4(ref, lo, hi, hs):
        return jnp.concatenate([ref[r, lo:hi, hs] for r in range(HUB)], axis=0)

    jobs = []
    for a in range(seq // BLK):
        for h, hs in enumerate(heads):
            lo, hi = a * piece, (a + 1) * piece
            klo = lo if a == 0 else lo - piece

            def scores(lo=lo, hi=hi, klo=klo, hs=hs):
                return _scores(gather4(q0_ref, lo, hi, hs), gather4(k0_ref, klo, hi, hs))

            def finish(s, a=a, h=h, hs=hs, lo=lo, hi=hi, klo=klo):
                new = _softmax_block(s, gather4(v0_ref, klo, hi, hs),
                                     mask0_first if a == 0 else mask0, None)
                for ref, val in zip(state_refs, new):
                    for r in range(HUB):
                        ref[h, r, lo:hi, :] = val[r * piece:(r + 1) * piece]

            jobs.append((scores, finish))
    for r in range(HUB):
        for n in range(seq // HUB // BLK):
            for h, hs in enumerate(heads):
                rows = slice(n * BLK, (n + 1) * BLK)
                krows = rows if n == 0 else slice((n - 1) * BLK, (n + 1) * BLK)

                def scores(r=r, rows=rows, krows=krows, hs=hs):
                    return _scores(q1_ref[r, rows, hs], k1_ref[r, krows, hs])

                def finish(s, r=r, n=n, h=h, hs=hs, rows=rows, krows=krows):
                    prev = tuple(ref[h, r, rows, :] for ref in state_refs)
                    new = _softmax_block(s, v1_ref[r, krows, hs], causal if n == 0 else band, prev)
                    for ref, val in zip(state_refs, new):
                        ref[h, r, rows, :] = val

                jobs.append((scores, finish))
    _run_pipelined(jobs)

    n16 = q2_ref.shape[0]
    for r16 in range(n16):
        r, s = r16 % HUB, r16 // HUB
        for h in range(HEADS_PER_STEP):
            for ref, ref16 in zip(state_refs, state16_refs):
                ref16[h, r16] = ref[h, r, pl.ds(s, BLK, stride=HUB), :]
    jobs = []
    for r16 in range(n16):
        for h, hs in enumerate(heads):

            def scores(r16=r16, hs=hs):
                return _scores(q2_ref[r16, :, hs], k2_ref[r16, :, hs])

            def finish(s, r16=r16, h=h, hs=hs):
                prev = tuple(ref16[h, r16] for ref16 in state16_refs)
                _, l, acc = _softmax_block(s, v2_ref[r16, :, hs], causal, prev)
                acc16_ref[h, r16] = acc * (1.0 / l)

            jobs.append((scores, finish))
    _run_pipelined(jobs)
    for r16 in range(n16):
        r, s = r16 % HUB, r16 // HUB
        for h in range(HEADS_PER_STEP):
            acc_ref[h, r, pl.ds(s, BLK, stride=HUB), :] = acc16_ref[h, r16]
    for r in range(HUB):
        for h, hs in enumerate(heads):
            y_ref[r, :, hs] = (acc_ref[h, r] * gate_ref[r, :, hs].astype(F32)).astype(y_ref.dtype)


def _attention(qk01, v01, qk2, v2, gate):
    batch, hub, res_len, _ = gate.shape
    d2, len2 = qk2.shape[1], qk2.shape[2]
    wc = HEADS_PER_STEP * HEAD_DIM
    cpm = MIX_WIDTH // wc
    spec4 = lambda slab: pl.BlockSpec((None, hub, res_len, wc),
                                      lambda b, hp: (b, 0, 0, slab * cpm + hp))
    spec16 = lambda slab: pl.BlockSpec((None, d2, len2, wc),
                                       lambda b, hp: (b, 0, 0, slab * cpm + hp))
    state = pltpu.VMEM((HEADS_PER_STEP, hub, res_len, LANES), F32)
    state16 = pltpu.VMEM((HEADS_PER_STEP, d2, len2, LANES), F32)
    return pl.pallas_call(
        _attn_kernel,
        grid=(batch, cpm),
        in_specs=[spec4(0), spec4(1), spec4(0),
                  spec4(2), spec4(3), spec4(1),
                  spec16(0), spec16(1), spec16(0),
                  spec4(0)],
        out_specs=spec4(0),
        out_shape=jax.ShapeDtypeStruct((batch, hub, res_len, MIX_WIDTH), BF16),
        scratch_shapes=[state, state, state, state16, state16, state16],
        compiler_params=_params(("parallel", "parallel")),
        name="dilated_attention",
    )(qk01, qk01, v01, qk01, qk01, v01, qk2, qk2, v2, gate)


def _residual_norm(out, g_ref, xo_ref, hn_ref):
    ms = jnp.mean(out * out, axis=-1, keepdims=True)
    nrm = out * lax.rsqrt(ms + RMS_EPS) * g_ref[...]
    if hn_ref is None:
        xo_ref[...] = nrm
    else:
        xo_ref[...] = out
        hn_ref[...] = nrm.astype(hn_ref.dtype)


def _a_out_kernel(y_ref, x_ref, w_ref, g_ref, xo_ref, hn_ref, stage_ref):
    hub, rows_per, d = y_ref.shape
    tm = hub * rows_per
    res = jnp.dot(y_ref[...].reshape(tm, d), w_ref[...], preferred_element_type=F32)
    for c in range(d // LANES):
        for r in range(hub):
            stage_ref[c, pl.ds(r, rows_per, stride=hub), :] = (
                res[r * rows_per:(r + 1) * rows_per, c * LANES:(c + 1) * LANES])
    sq = jnp.zeros((tm, LANES), F32)
    for c in range(d // LANES):
        cs = slice(c * LANES, (c + 1) * LANES)
        out = x_ref[:, cs] + stage_ref[c]
        xo_ref[:, cs] = out
        sq = sq + out * out
    inv = lax.rsqrt(jnp.sum(sq, axis=-1, keepdims=True) * (1.0 / d) + RMS_EPS)
    hn_ref[...] = (xo_ref[...] * inv * g_ref[...]).astype(hn_ref.dtype)


def _b_out_kernel(u_ref, v_ref, z_ref, x_ref, lng_ref, lnb_ref, ws_ref, bst_ref, w_ref, g_ref,
                  *rest, final):
    if final:
        xo_ref, vn_ref, y_ref = rest
        hn_ref = None
    else:
        xo_ref, hn_ref, vn_ref, y_ref = rest
    vf = v_ref[...].astype(F32)
    mu = jnp.mean(vf, axis=-1, keepdims=True)
    xc = vf - mu
    var = jnp.mean(xc * xc, axis=-1, keepdims=True)
    vn_ref[...] = (xc * lax.rsqrt(var + LN_EPS) * lng_ref[...] + lnb_ref[...]).astype(vn_ref.dtype)
    row = lax.broadcasted_iota(jnp.int32, (CHUNK, CHUNK), 0)
    col = lax.broadcasted_iota(jnp.int32, (CHUNK, CHUNK), 1)
    tril = row >= col
    for g in range(N_CH_GROUPS):
        gs = slice(g * CH_GROUP, (g + 1) * CH_GROUP)
        wm = jnp.where(tril, ws_ref[g], 0.0).astype(BF16)
        bias = bst_ref[:, g:g + 1]
        for n in range(u_ref.shape[0] // CHUNK):
            rows = slice(n * CHUNK, (n + 1) * CHUNK)
            sv = jnp.dot(wm, vn_ref[rows, gs], preferred_element_type=F32) + bias
            sgu = u_ref[rows, gs].astype(F32) * sv
            y_ref[rows, gs] = (sgu * z_ref[rows, gs].astype(F32)).astype(y_ref.dtype)
    out = x_ref[...] + jnp.dot(y_ref[...], w_ref[...], preferred_element_type=F32)
    _residual_norm(out, g_ref, xo_ref, hn_ref)


def _a_out(y4, x, w_out, g_next):
    t, d = x.shape
    batch, hub, res_len, _ = y4.shape
    tm = OUT_TM
    tiles_per_seq = hub * res_len // tm
    row_spec = pl.BlockSpec((tm, d), lambda i: (i, 0))
    return pl.pallas_call(
        _a_out_kernel,
        grid=(t // tm,),
        in_specs=[pl.BlockSpec((None, hub, tm // hub, d),
                               lambda i: (i // tiles_per_seq, 0, i % tiles_per_seq, 0)),
                  row_spec,
                  pl.BlockSpec((d, d), lambda i: (0, 0)),
                  pl.BlockSpec((1, d), lambda i: (0, 0))],
        out_specs=[row_spec, row_spec],
        out_shape=[jax.ShapeDtypeStruct((t, d), F32), jax.ShapeDtypeStruct((t, d), BF16)],
        scratch_shapes=[pltpu.VMEM((d // LANES, tm, LANES), F32)],
        compiler_params=_params(("parallel",)),
        name="a_outproj",
    )(y4, x, w_out.astype(BF16), g_next.reshape(1, d))


def _b_out(uv, z, x, ln_g, ln_b, w_s, b_s, w_out, g_next, final):
    t, d = x.shape
    tm = OUT_TM
    row_spec = pl.BlockSpec((tm, d), lambda i: (i, 0))
    vec_spec = pl.BlockSpec((1, d), lambda i: (0, 0))
    if final:
        out_specs, out_shape = [row_spec], [jax.ShapeDtypeStruct((t, d), F32)]
    else:
        out_specs = [row_spec, row_spec]
        out_shape = [jax.ShapeDtypeStruct((t, d), F32), jax.ShapeDtypeStruct((t, d), BF16)]
    return pl.pallas_call(
        functools.partial(_b_out_kernel, final=final),
        grid=(t // tm,),
        in_specs=[pl.BlockSpec((tm, d), lambda i: (i, 0)),
                  pl.BlockSpec((tm, d), lambda i: (i, 1)),
                  row_spec, row_spec, vec_spec, vec_spec,
                  pl.BlockSpec((N_CH_GROUPS, CHUNK, CHUNK), lambda i: (0, 0, 0)),
                  pl.BlockSpec((CHUNK, N_CH_GROUPS), lambda i: (0, 0)),
                  pl.BlockSpec((d, d), lambda i: (0, 0)),
                  vec_spec],
        out_specs=out_specs,
        out_shape=out_shape,
        scratch_shapes=[pltpu.VMEM((tm, d), BF16), pltpu.VMEM((tm, d), BF16)],
        compiler_params=_params(("parallel",)),
        name="b_gate_outproj",
    )(uv, uv, z, x, ln_g.reshape(1, d), ln_b.reshape(1, d), w_s, b_s.T,
      w_out.astype(BF16), g_next.reshape(1, d))


def kernel(x, positions, a_norm_g, a_w_in, a_w_out, b_norm_g, b_w_in, b_ln_g, b_ln_b,
           b_w_s, b_b_s, b_w_out, final_norm_g):
    batch, seq, d = x.shape
    depth = a_w_in.shape[0] + b_w_in.shape[0]
    xf = x.reshape(batch * seq, d)
    tables = _rope_tables(positions)
    hn = _rmsnorm(xf, a_norm_g[0])
    for layer in range(depth):
        j = layer // 2
        if layer % 2 == 0:
            y4 = _attention(*_a_inproj(hn, a_w_in, j, batch, tables))
            xf, hn = _a_out(y4, xf, a_w_out[j], b_norm_g[j])
        else:
            uv, z = _b_inproj(hn, b_w_in, j, batch)
            final = layer == depth - 1
            g_next = final_norm_g if final else a_norm_g[j + 1]
            outs = _b_out(uv, z, xf, b_ln_g[j], b_ln_b[j], b_w_s[j], b_b_s[j], b_w_out[j],
                          g_next, final)
            if final:
                xf = outs[0]
            else:
                xf, hn = outs
    return xf.reshape(batch, seq, d)
```
